```python
import jax, jax.numpy as jnp
from jax import lax
import numpy as np

D_MODEL = 1024
BATCH = 1
SEQ = 16384
DEPTH = 4

N_META = 16
D_MIX = D_MODEL
CONV_CH = D_MIX // 2
CONV_GROUPS = 8
CONV_WIDTH = 31
DN_HEADS = 4
DN_HEAD_DIM = 128
DN_KEY = DN_HEADS * DN_HEAD_DIM
DN_VAL = DN_HEADS * DN_HEAD_DIM
DN_SHORT_CONV = 4
CHUNK = 64
D_FF = -(-8 * D_MODEL // (3 * 256)) * 256
IN_SIZES = [CONV_CH, CONV_CH, DN_KEY, DN_KEY, DN_VAL, DN_VAL, DN_HEADS, DN_HEADS]
D_IN = sum(IN_SIZES)
IN_SPLITS = [int(s) for s in np.cumsum(IN_SIZES)[:-1]]
NORM_EPS = 1e-6
LN_EPS = 1e-5

kernel_name = "hymba_conformer_gated_deltanet_hybrid"


def rmsnorm(x, w, eps=NORM_EPS):
    xf = x.astype(jnp.float32)
    y = xf * lax.rsqrt(jnp.mean(xf * xf, axis=-1, keepdims=True) + eps)
    return (y * w.astype(jnp.float32)).astype(x.dtype)


def layernorm(x, w, b, eps=LN_EPS):
    xf = x.astype(jnp.float32)
    mu = jnp.mean(xf, axis=-1, keepdims=True)
    var = jnp.mean(jnp.square(xf - mu), axis=-1, keepdims=True)
    y = (xf - mu) * lax.rsqrt(var + eps) * w.astype(jnp.float32) + b.astype(jnp.float32)
    return y.astype(x.dtype)


def l2norm(x, eps=1e-6):
    return x * lax.rsqrt(jnp.sum(x * x, axis=-1, keepdims=True) + eps)


def causal_depthwise_conv(x, w):
    K, C = w.shape
    xp = jnp.pad(x, ((0, 0), (K - 1, 0), (0, 0)))
    return lax.conv_general_dilated(xp, w[:, None, :].astype(x.dtype), window_strides=(1,), padding='VALID',
                                    dimension_numbers=('NWC', 'WIO', 'NWC'), feature_group_count=C)


def conformer_conv_module(h_val, h_gate, dw_w, dw_b, ln_w, ln_b):
    u = h_val * jax.nn.sigmoid(h_gate)
    u = causal_depthwise_conv(u, dw_w) + dw_b
    u = layernorm(u, ln_w, ln_b)
    return jax.nn.silu(u)


def chunk_gated_delta_rule(q, k, v, g, beta):
    Bsz, L, H, dk = k.shape
    dv = v.shape[-1]
    front = CHUNK - N_META
    back = (-(L + front)) % CHUNK

    def to_chunks(t):
        t = jnp.pad(t, [(0, 0), (front, back)] + [(0, 0)] * (t.ndim - 2))
        t = jnp.moveaxis(t, 2, 1)
        return t.reshape((Bsz, H, -1, CHUNK) + t.shape[3:])

    q, k, v, g, beta = (to_chunks(t) for t in (q, k, v, g, beta))
    g = jnp.cumsum(g, axis=-1)
    causal = jnp.tril(jnp.ones((CHUNK, CHUNK), dtype=bool))
    strict = jnp.tril(jnp.ones((CHUNK, CHUNK), dtype=bool), -1)
    decay = jnp.exp(jnp.where(causal, g[..., :, None] - g[..., None, :], -jnp.inf))

    k_beta = k * beta[..., None]
    v_beta = v * beta[..., None]
    lower = jnp.where(strict, jnp.einsum('bhncd,bhnsd->bhncs', k_beta, k) * decay, 0.0)
    a_mat = lower + jnp.eye(CHUNK, dtype=lower.dtype)
    solve = lambda rhs: lax.linalg.triangular_solve(a_mat, rhs, left_side=True, lower=True, unit_diagonal=True)
    u = solve(v_beta)
    w = solve(k_beta * jnp.exp(g)[..., None])

    attn_intra = jnp.where(causal, jnp.einsum('bhncd,bhnsd->bhncs', q, k) * decay, 0.0)
    q_dec = q * jnp.exp(g)[..., None]
    k_dec = k * jnp.exp(g[..., -1:] - g)[..., None]
    g_last = jnp.exp(g[..., -1])

    def step(S, xs):
        q_i, k_i, u_i, w_i, a_i, gl_i = xs
        v_new = u_i - jnp.einsum('bhcd,bhdv->bhcv', w_i, S)
        o_i = jnp.einsum('bhcd,bhdv->bhcv', q_i, S) + jnp.einsum('bhcs,bhsv->bhcv', a_i, v_new)
        S = S * gl_i[..., None, None] + jnp.einsum('bhcd,bhcv->bhdv', k_i, v_new)
        return S, o_i

    xs = (jnp.moveaxis(q_dec, 2, 0), jnp.moveaxis(k_dec, 2, 0), jnp.moveaxis(u, 2, 0),
          jnp.moveaxis(w, 2, 0), jnp.moveaxis(attn_intra, 2, 0), jnp.moveaxis(g_last, 2, 0))
    S0 = jnp.zeros((Bsz, H, dk, dv), jnp.float32)
    _, o = lax.scan(step, S0, xs)
    o = jnp.transpose(o, (1, 0, 3, 2, 4)).reshape(Bsz, -1, H, dv)
    return o[:, front:front + L]


def gated_deltanet(q, k, v, z, b, a, conv_w, A_log, dt_bias, norm_w):
    Bsz, L, _ = q.shape
    dtype = v.dtype
    qkv = jax.nn.silu(causal_depthwise_conv(jnp.concatenate([q, k, v], axis=-1), conv_w))
    q, k, v = jnp.split(qkv.astype(jnp.float32), [DN_KEY, 2 * DN_KEY], axis=-1)
    q = l2norm(q.reshape(Bsz, L, DN_HEADS, DN_HEAD_DIM)) * (DN_HEAD_DIM ** -0.5)
    k = l2norm(k.reshape(Bsz, L, DN_HEADS, DN_HEAD_DIM))
    v = v.reshape(Bsz, L, DN_HEADS, DN_HEAD_DIM)
    beta = jax.nn.sigmoid(b.astype(jnp.float32))
    g = -jnp.exp(A_log.astype(jnp.float32)) * jax.nn.softplus(a.astype(jnp.float32) + dt_bias.astype(jnp.float32))
    o = chunk_gated_delta_rule(q, k, v, g, beta)
    o = o * lax.rsqrt(jnp.mean(o * o, axis=-1, keepdims=True) + NORM_EPS) * norm_w.astype(jnp.float32)
    o = o * jax.nn.silu(z.astype(jnp.float32).reshape(Bsz, L, DN_HEADS, DN_HEAD_DIM))
    return o.reshape(Bsz, L, DN_VAL).astype(dtype)


def setup_inputs(seed: int = 0) -> dict:
    key = jax.random.key(seed)
    ks = jax.random.split(key, 20)
    f = jnp.float32
    nrm = lambda k_, shape: jax.random.normal(k_, shape, f)
    dt = jnp.exp(jax.random.uniform(ks[10], (DEPTH, DN_HEADS), f, np.log(1e-3), np.log(1e-1)))
    return {
        "x": nrm(ks[0], (BATCH, SEQ, D_MODEL)),
        "meta_tokens": nrm(ks[1], (N_META, D_MODEL)),
        "norm_mix_w": 1.0 + 0.02 * nrm(ks[2], (DEPTH, D_MODEL)),
        "w_in": nrm(ks[3], (DEPTH, D_MODEL, D_IN)) * D_MODEL ** -0.5,
        "conv_dw_w": nrm(ks[4], (DEPTH, CONV_WIDTH, CONV_CH)) * CONV_WIDTH ** -0.5,
        "conv_dw_b": 0.02 * nrm(ks[5], (DEPTH, CONV_CH)),
        "conv_ln_w": 1.0 + 0.02 * nrm(ks[6], (DEPTH, CONV_CH)),
        "conv_ln_b": 0.02 * nrm(ks[7], (DEPTH, CONV_CH)),
        "dn_conv_w": nrm(ks[8], (DEPTH, DN_SHORT_CONV, 2 * DN_KEY + DN_VAL)) * DN_SHORT_CONV ** -0.5,
        "dn_A_log": jnp.log(jax.random.uniform(ks[9], (DEPTH, DN_HEADS), f, 1.0, 16.0)),
        "dn_dt_bias": dt + jnp.log(-jnp.expm1(-dt)),
        "dn_norm_w": 1.0 + 0.02 * nrm(ks[11], (DEPTH, DN_HEAD_DIM)),
        "w_out": nrm(ks[12], (DEPTH, D_MIX, D_MODEL)) * D_MIX ** -0.5,
        "norm_ffn_w": 1.0 + 0.02 * nrm(ks[13], (DEPTH, D_MODEL)),
        "ffn_w_gu": nrm(ks[14], (DEPTH, D_MODEL, 2 * D_FF)) * D_MODEL ** -0.5,
        "ffn_w_down": nrm(ks[15], (DEPTH, D_FF, D_MODEL)) * D_FF ** -0.5,
        "final_norm_w": 1.0 + 0.02 * nrm(ks[16], (D_MODEL,)),
    }


def reference(x, meta_tokens, norm_mix_w, w_in, conv_dw_w, conv_dw_b, conv_ln_w, conv_ln_b,
              dn_conv_w, dn_A_log, dn_dt_bias, dn_norm_w, w_out, norm_ffn_w, ffn_w_gu, ffn_w_down,
              final_norm_w):
    Bsz = x.shape[0]
    meta = jnp.broadcast_to(meta_tokens[None].astype(x.dtype), (Bsz, N_META, D_MODEL))
    h = jnp.concatenate([meta, x], axis=1)
    for l in range(DEPTH):
        hn = rmsnorm(h, norm_mix_w[l])
        proj = jnp.einsum('bld,de->ble', hn, w_in[l])
        c_val, c_gate, q, k, v, z, b, a = jnp.split(proj, IN_SPLITS, axis=-1)
        y_conv = conformer_conv_module(c_val, c_gate, conv_dw_w[l], conv_dw_b[l], conv_ln_w[l], conv_ln_b[l])
        y_dn = gated_deltanet(q, k, v, z, b, a, dn_conv_w[l], dn_A_log[l], dn_dt_bias[l], dn_norm_w[l])
        y = jnp.concatenate([y_conv, y_dn], axis=-1)
        h = h + jnp.einsum('ble,ed->bld', y, w_out[l])
        hn = rmsnorm(h, norm_ffn_w[l])
        gate, up = jnp.split(jnp.einsum('bld,df->blf', hn, ffn_w_gu[l]), 2, axis=-1)
        h = h + jnp.einsum('blf,fd->bld', jax.nn.silu(gate) * up, ffn_w_down[l])
    out = rmsnorm(h, final_norm_w)
    return out[:, N_META:]
```

```python
import functools

import jax
import jax.numpy as jnp
from jax import lax
from jax.experimental import pallas as pl
from jax.experimental.pallas import tpu as pltpu

D_MODEL = 1024
DEPTH = 4
N_META = 16
CHUNK = 64
FRONT = CHUNK - N_META
CONV_CH = 512
CONV_WIDTH = 31
DN_HEADS = 4
DN_HEAD_DIM = 128
DN_KEY = DN_HEADS * DN_HEAD_DIM
DN_SHORT_CONV = 4
D_FF = 2816
NORM_EPS = 1e-6
LN_EPS = 1e-5

LANES = 128
D_PROJ = 3072
D_IN_PAD = D_PROJ + LANES
COL_Q, COL_K, COL_V, COL_Z = 1024, 1536, 2048, 2560
FF_BLK = 256
N_FF_BLK = D_FF // FF_BLK
CONV_HALO = 32
QKV_HALO = 8

TM_PROJ = 512
TB_MIX = 512
VMEM_LIMIT = 56 * 1024 * 1024

F32 = jnp.float32
BF16 = jnp.bfloat16


def _dot(a, b):
    return jnp.dot(a, b, preferred_element_type=F32)


def _dot_nt(a, b):
    return lax.dot_general(a, b, (((1,), (1,)), ((), ())), preferred_element_type=F32)


def _rmsnorm(x, w):
    return x * lax.rsqrt(jnp.mean(x * x, axis=-1, keepdims=True) + NORM_EPS) * w


def _sigmoid(x):
    return 1.0 / (1.0 + jnp.exp(-x))


def _inproj_kernel(h_ref, nw_ref, w_ref, proj_ref, ba_ref):
    hn = _rmsnorm(h_ref[...], nw_ref[...]).astype(BF16)
    for c in range(D_PROJ // 256):
        proj_ref[:, c * 256:(c + 1) * 256] = _dot(hn, w_ref[:, c * 256:(c + 1) * 256])
    ba_ref[...] = _dot(hn, w_ref[:, D_PROJ:D_IN_PAD])


def _inproj(h, nw, w, tm):
    rows = h.shape[0]
    return pl.pallas_call(
        _inproj_kernel,
        grid=(rows // tm,),
        in_specs=[
            pl.BlockSpec((tm, D_MODEL), lambda i: (i, 0)),
            pl.BlockSpec((1, D_MODEL), lambda i: (0, 0)),
            pl.BlockSpec((D_MODEL, D_IN_PAD), lambda i: (0, 0), pipeline_mode=pl.Buffered(1)),
        ],
        out_specs=[
            pl.BlockSpec((tm, D_PROJ), lambda i: (i, 0)),
            pl.BlockSpec((tm, LANES), lambda i: (i, 0)),
        ],
        out_shape=[
            jax.ShapeDtypeStruct((rows, D_PROJ), F32),
            jax.ShapeDtypeStruct((rows, LANES), F32),
        ],
        compiler_params=pltpu.CompilerParams(
            dimension_semantics=("arbitrary",), vmem_limit_bytes=VMEM_LIMIT),
        name="inproj",
    )(h, nw, w)


def _ffn_kernel(h_ref, y_ref, wo_ref, nw_ref, wg_ref, wu_ref, wd_ref, fnw_ref, out_ref,
                acc_ref, hn_ref, *, final, zero_front):
    h1 = h_ref[...] + _dot(y_ref[...].astype(BF16), wo_ref[...])
    hn_ref[...] = _rmsnorm(h1, nw_ref[...]).astype(BF16)
    acc_ref[...] = h1

    def body(f, carry):
        hn = hn_ref[...]
        g = _dot(hn, wg_ref[f])
        u = _dot(hn, wu_ref[f])
        act = (g * _sigmoid(g) * u).astype(BF16)
        acc_ref[...] += _dot(act, wd_ref[f])
        return carry

    lax.fori_loop(0, N_FF_BLK, body, 0)
    out = acc_ref[...]
    if final:
        out = _rmsnorm(out, fnw_ref[...])
    if zero_front:
        row = lax.broadcasted_iota(jnp.int32, out.shape, 0)
        out = jnp.where(row >= FRONT, out, 0.0)
    out_ref[...] = out


def _ffn(h, y, wo, nw, wg, wu, wd, fnw, tm, final, zero_front):
    rows = h.shape[0]
    const3 = lambda i: (0, 0, 0)
    return pl.pallas_call(
        functools.partial(_ffn_kernel, final=final, zero_front=zero_front),
        grid=(rows // tm,),
        in_specs=[
            pl.BlockSpec((tm, D_MODEL), lambda i: (i, 0)),
            pl.BlockSpec((tm, D_MODEL), lambda i: (i, 0)),
            pl.BlockSpec((D_MODEL, D_MODEL), lambda i: (0, 0), pipeline_mode=pl.Buffered(1)),
            pl.BlockSpec((1, D_MODEL), lambda i: (0, 0)),
            pl.BlockSpec((N_FF_BLK, D_MODEL, FF_BLK), const3, pipeline_mode=pl.Buffered(1)),
            pl.BlockSpec((N_FF_BLK, D_MODEL, FF_BLK), const3, pipeline_mode=pl.Buffered(1)),
            pl.BlockSpec((N_FF_BLK, FF_BLK, D_MODEL), const3, pipeline_mode=pl.Buffered(1)),
            pl.BlockSpec((1, D_MODEL), lambda i: (0, 0)),
        ],
        out_specs=pl.BlockSpec((tm, D_MODEL), lambda i: (i, 0)),
        out_shape=jax.ShapeDtypeStruct((rows, D_MODEL), F32),
        scratch_shapes=[pltpu.VMEM((tm, D_MODEL), F32), pltpu.VMEM((tm, D_MODEL), BF16)],
        compiler_params=pltpu.CompilerParams(
            dimension_semantics=("arbitrary",), vmem_limit_bytes=VMEM_LIMIT),
        name="ffn",
    )(h, y, wo, nw, wg, wu, wd, fnw)


def _split3(x):
    a = x.astype(BF16)
    r = x - a.astype(F32)
    b = r.astype(BF16)
    c = (r - b.astype(F32)).astype(BF16)
    return a, b, c


def _mix_chunk(p_ref, ba_ref, y_ref, r0, consts, scratch, is_meta):
    (dww_ref, dwb_ref, lnw_ref, lnb_ref, cw_ref, alog_ref, dtb_ref, nrm_ref) = consts
    (uwin_ref, qwin_ref, s_ref) = scratch
    rows = pl.ds(r0, CHUNK)

    val = p_ref[rows, 0:CONV_CH]
    gate = p_ref[rows, CONV_CH:2 * CONV_CH]
    uwin_ref[CONV_HALO:CONV_HALO + CHUNK, :] = val * _sigmoid(gate)
    off = CONV_HALO - (CONV_WIDTH - 1)
    acc = jnp.zeros((CHUNK, CONV_CH), F32) + dwb_ref[...]
    for j in range(CONV_WIDTH):
        acc = acc + dww_ref[j:j + 1, :] * uwin_ref[off + j:off + j + CHUNK, :]
    uwin_ref[0:CONV_HALO, :] = uwin_ref[CHUNK:CHUNK + CONV_HALO, :]
    mu = jnp.mean(acc, axis=-1, keepdims=True)
    cen = acc - mu
    var = jnp.mean(cen * cen, axis=-1, keepdims=True)
    yc = cen * lax.rsqrt(var + LN_EPS) * lnw_ref[...] + lnb_ref[...]
    y_ref[rows, 0:CONV_CH] = yc * _sigmoid(yc)

    qwin_ref[QKV_HALO:QKV_HALO + CHUNK, :] = p_ref[rows, COL_Q:COL_Z]
    qoff = QKV_HALO - (DN_SHORT_CONV - 1)
    qkv = jnp.zeros((CHUNK, 3 * DN_KEY), F32)
    for j in range(DN_SHORT_CONV):
        qkv = qkv + cw_ref[j:j + 1, :] * qwin_ref[qoff + j:qoff + j + CHUNK, :]
    qwin_ref[0:QKV_HALO, :] = qwin_ref[CHUNK:CHUNK + QKV_HALO, :]
    qkv = qkv * _sigmoid(qkv)

    ba = ba_ref[rows, :]
    beta_all = _sigmoid(ba)
    xg = ba + dtb_ref[...]
    softplus = jnp.maximum(xg, 0.0) + jnp.log(1.0 + jnp.exp(-jnp.abs(xg)))
    g_all = -jnp.exp(alog_ref[...]) * softplus
    if is_meta:
        row = lax.broadcasted_iota(jnp.int32, g_all.shape, 0)
        g_all = jnp.where(row >= FRONT, g_all, 0.0)
    ci = lax.broadcasted_iota(jnp.int32, (CHUNK, CHUNK), 0)
    si = lax.broadcasted_iota(jnp.int32, (CHUNK, CHUNK), 1)
    causal = si <= ci
    strict = si < ci
    tri = jnp.where(causal, 1.0, 0.0).astype(BF16)
    g1, g2, g3 = _split3(g_all)
    gcs = _dot(tri, g1) + _dot(tri, g2) + _dot(tri, g3)
    gcs_t = gcs.T

    for h in range(DN_HEADS):
        lo = h * DN_HEAD_DIM
        q = qkv[:, lo:lo + DN_HEAD_DIM]
        k = qkv[:, DN_KEY + lo:DN_KEY + lo + DN_HEAD_DIM]
        v = qkv[:, 2 * DN_KEY + lo:2 * DN_KEY + lo + DN_HEAD_DIM]
        q = q * lax.rsqrt(jnp.sum(q * q, axis=-1, keepdims=True) + 1e-6) * (DN_HEAD_DIM ** -0.5)
        k = k * lax.rsqrt(jnp.sum(k * k, axis=-1, keepdims=True) + 1e-6)
        beta = beta_all[:, h:h + 1]
        gcol = gcs[:, DN_HEADS + h:DN_HEADS + h + 1]
        grow = gcs_t[DN_HEADS + h:DN_HEADS + h + 1, :]
        glast = gcs[CHUNK - 1:CHUNK, DN_HEADS + h:DN_HEADS + h + 1]
        decay = jnp.exp(jnp.where(causal, gcol - grow, -1e30))
        eg = jnp.exp(gcol)
        kb = k * beta
        kbf = k.astype(BF16)
        nmat = jnp.where(strict, _dot_nt(kb.astype(BF16), kbf) * decay, 0.0)
        attn = _dot_nt(q.astype(BF16), kbf) * decay

        x = jnp.concatenate([v * beta, kb * eg], axis=1)
        pw = nmat.astype(BF16)
        x = x - _dot(pw, x.astype(BF16))
        for _ in range(5):
            pw = _dot(pw, pw).astype(BF16)
            x = x + _dot(pw, x.astype(BF16))
        u = x[:, 0:DN_HEAD_DIM]
        w = x[:, DN_HEAD_DIM:2 * DN_HEAD_DIM]

        s_old = s_ref[h]
        sb = s_old.astype(BF16)
        v_new = u - _dot(w.astype(BF16), sb)
        vnb = v_new.astype(BF16)
        o = _dot((q * eg).astype(BF16), sb) + _dot(attn.astype(BF16), vnb)
        kd_t = (k * jnp.exp(glast - gcol)).T
        s_ref[h] = s_old * jnp.exp(glast) + _dot(kd_t.astype(BF16), vnb)

        z = p_ref[rows, COL_Z + lo:COL_Z + lo + DN_HEAD_DIM]
        o = o * lax.rsqrt(jnp.mean(o * o, axis=-1, keepdims=True) + NORM_EPS) * nrm_ref[...]
        y_ref[rows, CONV_CH + lo:CONV_CH + lo + DN_HEAD_DIM] = o * (z * _sigmoid(z))


def _mixer_kernel(pm_ref, bam_ref, p_ref, ba_ref, dww_ref, dwb_ref, lnw_ref, lnb_ref, cw_ref,
                  alog_ref, dtb_ref, nrm_ref, ym_ref, y_ref, uwin_ref, qwin_ref, s_ref):
    consts = (dww_ref, dwb_ref, lnw_ref, lnb_ref, cw_ref, alog_ref, dtb_ref, nrm_ref)
    scratch = (uwin_ref, qwin_ref, s_ref)

    @pl.when(pl.program_id(0) == 0)
    def _():
        uwin_ref[...] = jnp.zeros_like(uwin_ref)
        qwin_ref[...] = jnp.zeros_like(qwin_ref)
        s_ref[...] = jnp.zeros_like(s_ref)
        _mix_chunk(pm_ref, bam_ref, ym_ref, 0, consts, scratch, True)

    def body(c, carry):
        _mix_chunk(p_ref, ba_ref, y_ref, pl.multiple_of(c * CHUNK, CHUNK), consts, scratch, False)
        return carry

    lax.fori_loop(0, TB_MIX // CHUNK, body, 0)


def _mixer(proj_m, ba_m, proj, ba, dww, dwb, lnw, lnb, cw, alog, dtb, nrm):
    rows = proj.shape[0]
    c2 = lambda i: (0, 0)
    return pl.pallas_call(
        _mixer_kernel,
        grid=(rows // TB_MIX,),
        in_specs=[
            pl.BlockSpec((CHUNK, D_PROJ), c2),
            pl.BlockSpec((CHUNK, LANES), c2),
            pl.BlockSpec((TB_MIX, D_PROJ), lambda i: (i, 0)),
            pl.BlockSpec((TB_MIX, LANES), lambda i: (i, 0)),
            pl.BlockSpec((CONV_WIDTH + 1, CONV_CH), c2),
            pl.BlockSpec((1, CONV_CH), c2),
            pl.BlockSpec((1, CONV_CH), c2),
            pl.BlockSpec((1, CONV_CH), c2),
            pl.BlockSpec((2 * DN_SHORT_CONV, 3 * DN_KEY), c2),
            pl.BlockSpec((1, LANES), c2),
            pl.BlockSpec((1, LANES), c2),
            pl.BlockSpec((1, DN_HEAD_DIM), c2),
        ],
        out_specs=[
            pl.BlockSpec((CHUNK, D_MODEL), c2),
            pl.BlockSpec((TB_MIX, D_MODEL), lambda i: (i, 0)),
        ],
        out_shape=[
            jax.ShapeDtypeStruct((CHUNK, D_MODEL), F32),
            jax.ShapeDtypeStruct((rows, D_MODEL), F32),
        ],
        scratch_shapes=[
            pltpu.VMEM((CONV_HALO + CHUNK, CONV_CH), F32),
            pltpu.VMEM((QKV_HALO + CHUNK, 3 * DN_KEY), F32),
            pltpu.VMEM((DN_HEADS, DN_HEAD_DIM, DN_HEAD_DIM), F32),
        ],
        compiler_params=pltpu.CompilerParams(
            dimension_semantics=("arbitrary",), vmem_limit_bytes=VMEM_LIMIT),
        name="mixer",
    )(proj_m, ba_m, proj, ba, dww, dwb, lnw, lnb, cw, alog, dtb, nrm)


def kernel(x, meta_tokens, norm_mix_w, w_in, conv_dw_w, conv_dw_b, conv_ln_w, conv_ln_b, dn_conv_w,
           dn_A_log, dn_dt_bias, dn_norm_w, w_out, norm_ffn_w, ffn_w_gu, ffn_w_down, final_norm_w):
    assert x.shape[0] == 1
    h = x[0]
    hm = jnp.concatenate([jnp.zeros((FRONT, D_MODEL), F32), meta_tokens.astype(F32)], axis=0)

    w_in_p = jnp.pad(w_in, ((0, 0), (0, 0), (0, D_IN_PAD - w_in.shape[-1]))).astype(BF16)
    w_out_b = w_out.astype(BF16)
    wg = ffn_w_gu[:, :, :D_FF].reshape(DEPTH, D_MODEL, N_FF_BLK, FF_BLK).transpose(0, 2, 1, 3).astype(BF16)
    wu = ffn_w_gu[:, :, D_FF:].reshape(DEPTH, D_MODEL, N_FF_BLK, FF_BLK).transpose(0, 2, 1, 3).astype(BF16)
    wd = ffn_w_down.reshape(DEPTH, N_FF_BLK, FF_BLK, D_MODEL).astype(BF16)
    dww = jnp.pad(conv_dw_w, ((0, 0), (0, 1), (0, 0)))
    cw = jnp.pad(dn_conv_w, ((0, 0), (0, DN_SHORT_CONV), (0, 0)))
    head_pad = ((0, 0), (DN_HEADS, LANES - 2 * DN_HEADS))
    alog = jnp.pad(dn_A_log, head_pad)
    dtb = jnp.pad(dn_dt_bias, head_pad)
    fnw = final_norm_w.reshape(1, D_MODEL)

    for l in range(DEPTH):
        nw = norm_mix_w[l].reshape(1, D_MODEL)
        proj_m, ba_m = _inproj(hm, nw, w_in_p[l], CHUNK)
        proj, ba = _inproj(h, nw, w_in_p[l], TM_PROJ)
        ym, y = _mixer(proj_m, ba_m, proj, ba, dww[l], conv_dw_b[l].reshape(1, -1),
                       conv_ln_w[l].reshape(1, -1), conv_ln_b[l].reshape(1, -1), cw[l],
                       alog[l].reshape(1, -1), dtb[l].reshape(1, -1), dn_norm_w[l].reshape(1, -1))
        final = l == DEPTH - 1
        nfw = norm_ffn_w[l].reshape(1, D_MODEL)
        if not final:
            hm = _ffn(hm, ym, w_out_b[l], nfw, wg[l], wu[l], wd[l], fnw, CHUNK, False, True)
        h = _ffn(h, y, w_out_b[l], nfw, wg[l], wu[l], wd[l], fnw, TM_PROJ, final, False)
    return h[None]
```

```python
import functools

import jax
import jax.numpy as jnp
from jax import lax
from jax.experimental import pallas as pl
from jax.experimental.pallas import tpu as pltpu

D_MODEL = 1024
DEPTH = 4
N_META = 16
CHUNK = 64
FRONT = CHUNK - N_META
CONV_CH = 512
CONV_WIDTH = 31
DN_HEADS = 4
DN_HEAD_DIM = 128
DN_KEY = DN_HEADS * DN_HEAD_DIM
DN_SHORT_CONV = 4
D_FF = 2816
NORM_EPS = 1e-6
LN_EPS = 1e-5

LANES = 128
SUBLANES = 8
D_PROJ = 3072
D_IN_PAD = D_PROJ + LANES
COL_Q, COL_K, COL_V, COL_Z = 1024, 1536, 2048, 2560
FF_BLK = 256
N_FF_BLK = D_FF // FF_BLK
CONV_HALO = 32
QKV_HALO = 8
NEUMANN_STAGES = 6

TM_PROJ = 512
TB_MIX = 512
N_SLOT = TB_MIX // CHUNK
GROUP = 2
VMEM_LIMIT = 56 * 1024 * 1024

F32 = jnp.float32
BF16 = jnp.bfloat16


def _dot(a, b):
    return jnp.dot(a, b, preferred_element_type=F32)


def _dot_nt(a, b):
    return lax.dot_general(a, b, (((1,), (1,)), ((), ())), preferred_element_type=F32)


def _rmsnorm(x, w):
    return x * lax.rsqrt(jnp.mean(x * x, axis=-1, keepdims=True) + NORM_EPS) * w


def _sigmoid(x):
    return 1.0 / (1.0 + jnp.exp(-x))


def _inproj_kernel(h_ref, nw_ref, w_ref, proj_ref, ba_ref):
    hn = _rmsnorm(h_ref[...], nw_ref[...]).astype(BF16)
    for c in range(D_PROJ // 256):
        proj_ref[:, c * 256:(c + 1) * 256] = _dot(hn, w_ref[:, c * 256:(c + 1) * 256])
    ba_ref[...] = _dot(hn, w_ref[:, D_PROJ:D_IN_PAD])


def _inproj(h, nw, w, tm):
    rows = h.shape[0]
    return pl.pallas_call(
        _inproj_kernel,
        grid=(rows // tm,),
        in_specs=[
            pl.BlockSpec((tm, D_MODEL), lambda i: (i, 0)),
            pl.BlockSpec((1, D_MODEL), lambda i: (0, 0)),
            pl.BlockSpec((D_MODEL, D_IN_PAD), lambda i: (0, 0), pipeline_mode=pl.Buffered(1)),
        ],
        out_specs=[
            pl.BlockSpec((tm, D_PROJ), lambda i: (i, 0)),
            pl.BlockSpec((tm, LANES), lambda i: (i, 0)),
        ],
        out_shape=[
            jax.ShapeDtypeStruct((rows, D_PROJ), F32),
            jax.ShapeDtypeStruct((rows, LANES), F32),
        ],
        compiler_params=pltpu.CompilerParams(
            dimension_semantics=("arbitrary",), vmem_limit_bytes=VMEM_LIMIT),
        name="inproj",
    )(h, nw, w)


def _ffn_kernel(h_ref, y_ref, wo_ref, nw_ref, wg_ref, wu_ref, wd_ref, fnw_ref, out_ref,
                acc_ref, hn_ref, *, final, zero_front):
    h1 = h_ref[...] + _dot(y_ref[...].astype(BF16), wo_ref[...])
    hn_ref[...] = _rmsnorm(h1, nw_ref[...]).astype(BF16)
    acc_ref[...] = h1

    def body(f, carry):
        hn = hn_ref[...]
        g = _dot(hn, wg_ref[f])
        u = _dot(hn, wu_ref[f])
        act = (g * _sigmoid(g) * u).astype(BF16)
        acc_ref[...] += _dot(act, wd_ref[f])
        return carry

    lax.fori_loop(0, N_FF_BLK, body, 0)
    out = acc_ref[...]
    if final:
        out = _rmsnorm(out, fnw_ref[...])
    if zero_front:
        row = lax.broadcasted_iota(jnp.int32, out.shape, 0)
        out = jnp.where(row >= FRONT, out, 0.0)
    out_ref[...] = out


def _ffn(h, y, wo, nw, wg, wu, wd, fnw, tm, final, zero_front):
    rows = h.shape[0]
    const3 = lambda i: (0, 0, 0)
    return pl.pallas_call(
        functools.partial(_ffn_kernel, final=final, zero_front=zero_front),
        grid=(rows // tm,),
        in_specs=[
            pl.BlockSpec((tm, D_MODEL), lambda i: (i, 0)),
            pl.BlockSpec((tm, D_MODEL), lambda i: (i, 0)),
            pl.BlockSpec((D_MODEL, D_MODEL), lambda i: (0, 0), pipeline_mode=pl.Buffered(1)),
            pl.BlockSpec((1, D_MODEL), lambda i: (0, 0)),
            pl.BlockSpec((N_FF_BLK, D_MODEL, FF_BLK), const3, pipeline_mode=pl.Buffered(1)),
            pl.BlockSpec((N_FF_BLK, D_MODEL, FF_BLK), const3, pipeline_mode=pl.Buffered(1)),
            pl.BlockSpec((N_FF_BLK, FF_BLK, D_MODEL), const3, pipeline_mode=pl.Buffered(1)),
            pl.BlockSpec((1, D_MODEL), lambda i: (0, 0)),
        ],
        out_specs=pl.BlockSpec((tm, D_MODEL), lambda i: (i, 0)),
        out_shape=jax.ShapeDtypeStruct((rows, D_MODEL), F32),
        scratch_shapes=[pltpu.VMEM((tm, D_MODEL), F32), pltpu.VMEM((tm, D_MODEL), BF16)],
        compiler_params=pltpu.CompilerParams(
            dimension_semantics=("arbitrary",), vmem_limit_bytes=VMEM_LIMIT),
        name="ffn",
    )(h, y, wo, nw, wg, wu, wd, fnw)


def _split3(x):
    a = x.astype(BF16)
    r = x - a.astype(F32)
    b = r.astype(BF16)
    c = (r - b.astype(F32)).astype(BF16)
    return a, b, c


def _conv_module(p_ref, y_ref, rows, consts, uwin_ref):
    dww_ref, dwb_ref, lnw_ref, lnb_ref = consts
    val = p_ref[rows, 0:CONV_CH]
    gate = p_ref[rows, CONV_CH:2 * CONV_CH]
    uwin_ref[CONV_HALO:CONV_HALO + CHUNK, :] = val * _sigmoid(gate)
    off = CONV_HALO - (CONV_WIDTH - 1)
    acc = jnp.zeros((CHUNK, CONV_CH), F32) + dwb_ref[...]
    for s in range(SUBLANES):
        nrow = CHUNK if s == 0 else CHUNK + SUBLANES
        part = None
        for j in range(CONV_WIDTH):
            if (off + j) % SUBLANES != s:
                continue
            a = (off + j) // SUBLANES * SUBLANES
            term = dww_ref[j:j + 1, :] * uwin_ref[a:a + nrow, :]
            part = term if part is None else part + term
        acc = acc + part[s:s + CHUNK, :]
    uwin_ref[0:CONV_HALO, :] = uwin_ref[CHUNK:CHUNK + CONV_HALO, :]
    mu = jnp.mean(acc, axis=-1, keepdims=True)
    cen = acc - mu
    var = jnp.mean(cen * cen, axis=-1, keepdims=True)
    yc = cen * lax.rsqrt(var + LN_EPS) * lnw_ref[...] + lnb_ref[...]
    y_ref[rows, 0:CONV_CH] = yc * _sigmoid(yc)


def _prep_chunk(p_ref, ba_ref, rows, consts, qwin_ref, is_meta):
    cw_ref, alog_ref, dtb_ref = consts
    qwin_ref[QKV_HALO:QKV_HALO + CHUNK, :] = p_ref[rows, COL_Q:COL_Z]
    qoff = QKV_HALO - (DN_SHORT_CONV - 1)
    qkv = jnp.zeros((CHUNK, 3 * DN_KEY), F32)
    for j in range(DN_SHORT_CONV):
        qkv = qkv + cw_ref[j:j + 1, :] * qwin_ref[qoff + j:qoff + j + CHUNK, :]
    qwin_ref[0:QKV_HALO, :] = qwin_ref[CHUNK:CHUNK + QKV_HALO, :]
    qkv = qkv * _sigmoid(qkv)

    ba = ba_ref[rows, :]
    beta_all = _sigmoid(ba)
    xg = ba + dtb_ref[...]
    softplus = jnp.maximum(xg, 0.0) + jnp.log(1.0 + jnp.exp(-jnp.abs(xg)))
    g_all = -jnp.exp(alog_ref[...]) * softplus
    if is_meta:
        row = lax.broadcasted_iota(jnp.int32, g_all.shape, 0)
        g_all = jnp.where(row >= FRONT, g_all, 0.0)
    ci = lax.broadcasted_iota(jnp.int32, (CHUNK, CHUNK), 0)
    si = lax.broadcasted_iota(jnp.int32, (CHUNK, CHUNK), 1)
    causal = si <= ci
    strict = si < ci
    tri = jnp.where(causal, 1.0, 0.0).astype(BF16)
    g1, g2, g3 = _split3(g_all)
    gcs = _dot(tri, g1) + _dot(tri, g2) + _dot(tri, g3)
    gcs_t = gcs.T

    heads = []
    for h in range(DN_HEADS):
        lo = h * DN_HEAD_DIM
        q = qkv[:, lo:lo + DN_HEAD_DIM]
        k = qkv[:, DN_KEY + lo:DN_KEY + lo + DN_HEAD_DIM]
        v = qkv[:, 2 * DN_KEY + lo:2 * DN_KEY + lo + DN_HEAD_DIM]
        q = q * lax.rsqrt(jnp.sum(q * q, axis=-1, keepdims=True) + 1e-6) * (DN_HEAD_DIM ** -0.5)
        k = k * lax.rsqrt(jnp.sum(k * k, axis=-1, keepdims=True) + 1e-6)
        beta = beta_all[:, h:h + 1]
        gcol = gcs[:, DN_HEADS + h:DN_HEADS + h + 1]
        grow = gcs_t[DN_HEADS + h:DN_HEADS + h + 1, :]
        glast = gcs[CHUNK - 1:CHUNK, DN_HEADS + h:DN_HEADS + h + 1]
        decay = jnp.exp(jnp.where(causal, gcol - grow, -1e30))
        eg = jnp.exp(gcol)
        kb = k * beta
        heads.append(dict(
            q=q, k=k, kb=kb, decay=decay, strict=strict,
            x=jnp.concatenate([v * beta, kb * eg], axis=1),
            qd=q * eg,
            kd_t=(k * jnp.exp(glast - gcol)).T,
            gl=jnp.broadcast_to(jnp.exp(glast), (1, LANES)),
        ))
    return heads


def _solve_chains(chains, slots, pq_ref, bb_ref, oo_ref, gl_ref):
    n = len(chains)
    pk = [_dot_nt(jnp.concatenate([c["kb"], c["q"]], axis=0).astype(BF16), c["k"].astype(BF16))
          for c in chains]
    pw, attn, x = [], [], []
    for c, r in zip(chains, pk):
        pw.append(jnp.where(c["strict"], r[0:CHUNK] * c["decay"], 0.0).astype(BF16))
        attn.append(r[CHUNK:2 * CHUNK] * c["decay"])
        x.append(c["x"])
    for s in range(NEUMANN_STAGES):
        last = s == NEUMANN_STAGES - 1
        if last:
            r = [_dot(pw[i], x[i].astype(BF16)) for i in range(n)]
        else:
            r = [_dot(pw[i], jnp.concatenate([x[i].astype(BF16), pw[i]], axis=1)) for i in range(n)]
        for i in range(n):
            upd = r[i][:, 0:2 * DN_HEAD_DIM]
            x[i] = x[i] - upd if s == 0 else x[i] + upd
            if not last:
                pw[i] = r[i][:, 2 * DN_HEAD_DIM:2 * DN_HEAD_DIM + CHUNK].astype(BF16)
    fin = []
    for i, c in enumerate(chains):
        wu = jnp.concatenate([x[i][:, DN_HEAD_DIM:], x[i][:, :DN_HEAD_DIM]], axis=1).astype(BF16)
        lhs = jnp.concatenate([c["kd_t"], attn[i]], axis=0).astype(BF16)
        fin.append(_dot(lhs, wu))
    for i, c in enumerate(chains):
        slot, h = slots[i]
        r = fin[i]
        p_mat = r[0:DN_HEAD_DIM, 0:DN_HEAD_DIM]
        q_eff = c["qd"] - r[DN_HEAD_DIM:, 0:DN_HEAD_DIM]
        pq_ref[slot, h] = jnp.concatenate([p_mat, q_eff], axis=0).astype(BF16)
        bb_ref[slot, h] = r[0:DN_HEAD_DIM, DN_HEAD_DIM:]
        oo_ref[slot, h] = r[DN_HEAD_DIM:, DN_HEAD_DIM:]
        gl_ref[slot, h:h + 1, :] = c["gl"]


def _state_step(p_ref, y_ref, rows, slot, nrm_ref, pq_ref, bb_ref, oo_ref, gl_ref, s_ref):
    s_old = [s_ref[h] for h in range(DN_HEADS)]
    r = [_dot(pq_ref[slot, h], s_old[h].astype(BF16)) for h in range(DN_HEADS)]
    for h in range(DN_HEADS):
        lo = h * DN_HEAD_DIM
        gl = gl_ref[slot, h:h + 1, :]
        s_ref[h] = s_old[h] * gl + bb_ref[slot, h] - r[h][0:DN_HEAD_DIM]
        o = r[h][DN_HEAD_DIM:] + oo_ref[slot, h]
        z = p_ref[rows, COL_Z + lo:COL_Z + lo + DN_HEAD_DIM]
        o = o * lax.rsqrt(jnp.mean(o * o, axis=-1, keepdims=True) + NORM_EPS) * nrm_ref[...]
        y_ref[rows, CONV_CH + lo:CONV_CH + lo + DN_HEAD_DIM] = o * (z * _sigmoid(z))


def _mixer_kernel(pm_ref, bam_ref, p_ref, ba_ref, dww_ref, dwb_ref, lnw_ref, lnb_ref, cw_ref,
                  alog_ref, dtb_ref, nrm_ref, ym_ref, y_ref,
                  uwin_ref, qwin_ref, s_ref, pq_ref, bb_ref, oo_ref, gl_ref):
    conv_consts = (dww_ref, dwb_ref, lnw_ref, lnb_ref)
    prep_consts = (cw_ref, alog_ref, dtb_ref)
    solved = (pq_ref, bb_ref, oo_ref, gl_ref)

    @pl.when(pl.program_id(0) == 0)
    def _():
        uwin_ref[...] = jnp.zeros_like(uwin_ref)
        qwin_ref[...] = jnp.zeros_like(qwin_ref)
        s_ref[...] = jnp.zeros_like(s_ref)
        rows = pl.ds(0, CHUNK)
        _conv_module(pm_ref, ym_ref, rows, conv_consts, uwin_ref)
        heads = _prep_chunk(pm_ref, bam_ref, rows, prep_consts, qwin_ref, True)
        _solve_chains(heads, [(0, h) for h in range(DN_HEADS)], *solved)
        _state_step(pm_ref, ym_ref, rows, 0, nrm_ref, *solved, s_ref)

    def solve_body(g, carry):
        chains, slots = [], []
        for i in range(GROUP):
            slot = g * GROUP + i
            rows = pl.ds(pl.multiple_of(slot * CHUNK, CHUNK), CHUNK)
            _conv_module(p_ref, y_ref, rows, conv_consts, uwin_ref)
            chains += _prep_chunk(p_ref, ba_ref, rows, prep_consts, qwin_ref, False)
            slots += [(slot, h) for h in range(DN_HEADS)]
        _solve_chains(chains, slots, *solved)
        return carry

    lax.fori_loop(0, N_SLOT // GROUP, solve_body, 0)

    def state_body(c, carry):
        rows = pl.ds(pl.multiple_of(c * CHUNK, CHUNK), CHUNK)
        _state_step(p_ref, y_ref, rows, c, nrm_ref, *solved, s_ref)
        return carry

    lax.fori_loop(0, N_SLOT, state_body, 0)


def _mixer(proj_m, ba_m, proj, ba, dww, dwb, lnw, lnb, cw, alog, dtb, nrm):
    rows = proj.shape[0]
    c2 = lambda i: (0, 0)
    return pl.pallas_call(
        _mixer_kernel,
        grid=(rows // TB_MIX,),
        in_specs=[
            pl.BlockSpec((CHUNK, D_PROJ), c2),
            pl.BlockSpec((CHUNK, LANES), c2),
            pl.BlockSpec((TB_MIX, D_PROJ), lambda i: (i, 0)),
            pl.BlockSpec((TB_MIX, LANES), lambda i: (i, 0)),
            pl.BlockSpec((CONV_WIDTH + 1, CONV_CH), c2),
            pl.BlockSpec((1, CONV_CH), c2),
            pl.BlockSpec((1, CONV_CH), c2),
            pl.BlockSpec((1, CONV_CH), c2),
            pl.BlockSpec((2 * DN_SHORT_CONV, 3 * DN_KEY), c2),
            pl.BlockSpec((1, LANES), c2),
            pl.BlockSpec((1, LANES), c2),
            pl.BlockSpec((1, DN_HEAD_DIM), c2),
        ],
        out_specs=[
            pl.BlockSpec((CHUNK, D_MODEL), c2),
            pl.BlockSpec((TB_MIX, D_MODEL), lambda i: (i, 0)),
        ],
        out_shape=[
            jax.ShapeDtypeStruct((CHUNK, D_MODEL), F32),
            jax.ShapeDtypeStruct((rows, D_MODEL), F32),
        ],
        scratch_shapes=[
            pltpu.VMEM((CONV_HALO + CHUNK, CONV_CH), F32),
            pltpu.VMEM((QKV_HALO + CHUNK, 3 * DN_KEY), F32),
            pltpu.VMEM((DN_HEADS, DN_HEAD_DIM, DN_HEAD_DIM), F32),
            pltpu.VMEM((N_SLOT, DN_HEADS, DN_HEAD_DIM + CHUNK, DN_HEAD_DIM), BF16),
            pltpu.VMEM((N_SLOT, DN_HEADS, DN_HEAD_DIM, DN_HEAD_DIM), F32),
            pltpu.VMEM((N_SLOT, DN_HEADS, CHUNK, DN_HEAD_DIM), F32),
            pltpu.VMEM((N_SLOT, SUBLANES, LANES), F32),
        ],
        compiler_params=pltpu.CompilerParams(
            dimension_semantics=("arbitrary",), vmem_limit_bytes=VMEM_LIMIT),
        name="mixer",
    )(proj_m, ba_m, proj, ba, dww, dwb, lnw, lnb, cw, alog, dtb, nrm)


def kernel(x, meta_tokens, norm_mix_w, w_in, conv_dw_w, conv_dw_b, conv_ln_w, conv_ln_b, dn_conv_w,
           dn_A_log, dn_dt_bias, dn_norm_w, w_out, norm_ffn_w, ffn_w_gu, ffn_w_down, final_norm_w):
    assert x.shape[0] == 1
    h = x[0]
    hm = jnp.concatenate([jnp.zeros((FRONT, D_MODEL), F32), meta_tokens.astype(F32)], axis=0)

    w_in_p = jnp.pad(w_in, ((0, 0), (0, 0), (0, D_IN_PAD - w_in.shape[-1]))).astype(BF16)
    w_out_b = w_out.astype(BF16)
    wg = ffn_w_gu[:, :, :D_FF].reshape(DEPTH, D_MODEL, N_FF_BLK, FF_BLK).transpose(0, 2, 1, 3).astype(BF16)
    wu = ffn_w_gu[:, :, D_FF:].reshape(DEPTH, D_MODEL, N_FF_BLK, FF_BLK).transpose(0, 2, 1, 3).astype(BF16)
    wd = ffn_w_down.reshape(DEPTH, N_FF_BLK, FF_BLK, D_MODEL).astype(BF16)
    dww = jnp.pad(conv_dw_w, ((0, 0), (0, 1), (0, 0)))
    cw = jnp.pad(dn_conv_w, ((0, 0), (0, DN_SHORT_CONV), (0, 0)))
    head_pad = ((0, 0), (DN_HEADS, LANES - 2 * DN_HEADS))
    alog = jnp.pad(dn_A_log, head_pad)
    dtb = jnp.pad(dn_dt_bias, head_pad)
    fnw = final_norm_w.reshape(1, D_MODEL)

    for l in range(DEPTH):
        nw = norm_mix_w[l].reshape(1, D_MODEL)
        proj_m, ba_m = _inproj(hm, nw, w_in_p[l], CHUNK)
        proj, ba = _inproj(h, nw, w_in_p[l], TM_PROJ)
        ym, y = _mixer(proj_m, ba_m, proj, ba, dww[l], conv_dw_b[l].reshape(1, -1),
                       conv_ln_w[l].reshape(1, -1), conv_ln_b[l].reshape(1, -1), cw[l],
                       alog[l].reshape(1, -1), dtb[l].reshape(1, -1), dn_norm_w[l].reshape(1, -1))
        final = l == DEPTH - 1
        nfw = norm_ffn_w[l].reshape(1, D_MODEL)
        if not final:
            hm = _ffn(hm, ym, w_out_b[l], nfw, wg[l], wu[l], wd[l], fnw, CHUNK, False, True)
        h = _ffn(h, y, wo=w_out_b[l], nw=nfw, wg=wg[l], wu=wu[l], wd=wd[l], fnw=fnw, tm=TM_PROJ,
                 final=final, zero_front=False)
    return h[None]
```

```python
import functools

import jax
import jax.numpy as jnp
from jax import lax
from jax.experimental import pallas as pl
from jax.experimental.pallas import tpu as pltpu

D_MODEL = 1024
DEPTH = 4
N_META = 16
CHUNK = 64
BLK = 2 * CHUNK
FRONT = BLK - N_META
CONV_CH = 512
CONV_WIDTH = 31
DN_HEADS = 4
DN_HEAD_DIM = 128
DN_KEY = DN_HEADS * DN_HEAD_DIM
DN_SHORT_CONV = 4
D_FF = 2816
NORM_EPS = 1e-6
LN_EPS = 1e-5

LANES = 128
SUBLANES = 8
MXU_COLS = 256
D_QKVZ = 4 * DN_KEY
D_PROJ = 2 * CONV_CH + D_QKVZ
D_IN_PAD = D_PROJ + LANES
COL_K, COL_V, COL_Z = DN_KEY, 2 * DN_KEY, 3 * DN_KEY
FF_BLK = MXU_COLS
N_FF_BLK = D_FF // FF_BLK
CONV_HALO = 32
CONV_ROWS = 64
QKV_HALO = 8
NEUMANN_STAGES = 6

TM_PROJ = 512
TB_MIX = 512
N_SLOT = TB_MIX // BLK
N_SUB = TB_MIX // CHUNK
GROUP = 2
STATE_UNROLL = 2
VMEM_LIMIT = 56 * 1024 * 1024

F32 = jnp.float32
BF16 = jnp.bfloat16


def _dot(a, b):
    return jnp.dot(a, b, preferred_element_type=F32)


def _rmsnorm(x, w):
    return x * lax.rsqrt(jnp.mean(x * x, axis=-1, keepdims=True) + NORM_EPS) * w


def _sigmoid(x):
    return 1.0 / (1.0 + jnp.exp(-x))


def _conv_rows(uwin_ref, consts, r0):
    dww_ref, dwb_ref, lnw_ref, lnb_ref = consts
    off = CONV_HALO - (CONV_WIDTH - 1)
    acc = jnp.zeros((CONV_ROWS, CONV_CH), F32) + dwb_ref[...]
    for s in range(SUBLANES):
        nrow = CONV_ROWS if s == 0 else CONV_ROWS + SUBLANES
        part = None
        for j in range(CONV_WIDTH):
            if (off + j) % SUBLANES != s:
                continue
            a = r0 + (off + j) // SUBLANES * SUBLANES
            term = dww_ref[j:j + 1, :] * uwin_ref[a:a + nrow, :]
            part = term if part is None else part + term
        acc = acc + part[s:s + CONV_ROWS, :]
    mu = jnp.mean(acc, axis=-1, keepdims=True)
    cen = acc - mu
    var = jnp.mean(cen * cen, axis=-1, keepdims=True)
    yc = cen * lax.rsqrt(var + LN_EPS) * lnw_ref[...] + lnb_ref[...]
    return yc * _sigmoid(yc)


def _inproj_rows(h_ref, nw_ref, w_ref, consts, yc_ref, qkvz_ref, ba_ref, uwin_ref, n):
    hn = _rmsnorm(h_ref[...], nw_ref[...]).astype(BF16)
    val = _dot(hn, w_ref[:, 0:CONV_CH])
    gate = _dot(hn, w_ref[:, CONV_CH:2 * CONV_CH])
    for c in range(D_QKVZ // MXU_COLS):
        lo = 2 * CONV_CH + c * MXU_COLS
        qkvz_ref[:, c * MXU_COLS:(c + 1) * MXU_COLS] = _dot(hn, w_ref[:, lo:lo + MXU_COLS])
    ba_ref[...] = _dot(hn, w_ref[:, D_PROJ:D_IN_PAD])
    uwin_ref[CONV_HALO:CONV_HALO + n, :] = val * _sigmoid(gate)
    for b in range(n // CONV_ROWS):
        r0 = b * CONV_ROWS
        yc_ref[r0:r0 + CONV_ROWS, :] = _conv_rows(uwin_ref, consts, r0).astype(BF16)
    uwin_ref[0:CONV_HALO, :] = uwin_ref[n:n + CONV_HALO, :]


def _inproj_kernel(hm_ref, h_ref, nw_ref, w_ref, dww_ref, dwb_ref, lnw_ref, lnb_ref,
                   ycm_ref, qkvzm_ref, bam_ref, yc_ref, qkvz_ref, ba_ref, uwin_ref):
    consts = (dww_ref, dwb_ref, lnw_ref, lnb_ref)

    @pl.when(pl.program_id(0) == 0)
    def _():
        uwin_ref[0:CONV_HALO, :] = jnp.zeros((CONV_HALO, CONV_CH), F32)
        _inproj_rows(hm_ref, nw_ref, w_ref, consts, ycm_ref, qkvzm_ref, bam_ref, uwin_ref, BLK)

    _inproj_rows(h_ref, nw_ref, w_ref, consts, yc_ref, qkvz_ref, ba_ref, uwin_ref, TM_PROJ)


def _inproj(hm, h, nw, w, dww, dwb, lnw, lnb):
    rows = h.shape[0]
    c2 = lambda i: (0, 0)
    tile = lambda i: (i, 0)
    return pl.pallas_call(
        _inproj_kernel,
        grid=(rows // TM_PROJ,),
        in_specs=[
            pl.BlockSpec((BLK, D_MODEL), c2),
            pl.BlockSpec((TM_PROJ, D_MODEL), tile),
            pl.BlockSpec((1, D_MODEL), c2),
            pl.BlockSpec((D_MODEL, D_IN_PAD), c2, pipeline_mode=pl.Buffered(1)),
            pl.BlockSpec((CONV_WIDTH + 1, CONV_CH), c2),
            pl.BlockSpec((1, CONV_CH), c2),
            pl.BlockSpec((1, CONV_CH), c2),
            pl.BlockSpec((1, CONV_CH), c2),
        ],
        out_specs=[
            pl.BlockSpec((BLK, CONV_CH), c2),
            pl.BlockSpec((BLK, D_QKVZ), c2),
            pl.BlockSpec((BLK, LANES), c2),
            pl.BlockSpec((TM_PROJ, CONV_CH), tile),
            pl.BlockSpec((TM_PROJ, D_QKVZ), tile),
            pl.BlockSpec((TM_PROJ, LANES), tile),
        ],
        out_shape=[
            jax.ShapeDtypeStruct((BLK, CONV_CH), BF16),
            jax.ShapeDtypeStruct((BLK, D_QKVZ), F32),
            jax.ShapeDtypeStruct((BLK, LANES), F32),
            jax.ShapeDtypeStruct((rows, CONV_CH), BF16),
            jax.ShapeDtypeStruct((rows, D_QKVZ), F32),
            jax.ShapeDtypeStruct((rows, LANES), F32),
        ],
        scratch_shapes=[pltpu.VMEM((CONV_HALO + TM_PROJ, CONV_CH), F32)],
        compiler_params=pltpu.CompilerParams(
            dimension_semantics=("arbitrary",), vmem_limit_bytes=VMEM_LIMIT),
        name="inproj",
    )(hm, h, nw, w, dww, dwb, lnw, lnb)


def _ffn_rows(h_ref, yc_ref, yd_ref, weights, out_ref, acc_ref, hn_ref, n, final, zero_front):
    woc_ref, wod_ref, nw_ref, wg_ref, wu_ref, wd_ref, fnw_ref = weights
    h1 = h_ref[...] + _dot(yc_ref[...], woc_ref[...]) + _dot(yd_ref[...], wod_ref[...])
    hn_ref[0:n, :] = _rmsnorm(h1, nw_ref[...]).astype(BF16)
    acc_ref[0:n, :] = h1

    def body(f, carry):
        hn = hn_ref[0:n, :]
        g = _dot(hn, wg_ref[f])
        u = _dot(hn, wu_ref[f])
        act = (g * _sigmoid(g) * u).astype(BF16)
        acc_ref[0:n, :] += _dot(act, wd_ref[f])
        return carry

    lax.fori_loop(0, N_FF_BLK, body, 0)
    out = acc_ref[0:n, :]
    if final:
        out = _rmsnorm(out, fnw_ref[...])
    if zero_front:
        row = lax.broadcasted_iota(jnp.int32, out.shape, 0)
        out = jnp.where(row >= FRONT, out, 0.0)
    out_ref[...] = out


def _ffn_kernel(*refs, final):
    if final:
        (h_ref, yc_ref, yd_ref, *weights, out_ref, acc_ref, hn_ref) = refs
    else:
        (hm_ref, ycm_ref, ydm_ref, h_ref, yc_ref, yd_ref, *weights, outm_ref, out_ref,
         acc_ref, hn_ref) = refs

        @pl.when(pl.program_id(0) == 0)
        def _():
            _ffn_rows(hm_ref, ycm_ref, ydm_ref, weights, outm_ref, acc_ref, hn_ref, BLK, False, True)

    _ffn_rows(h_ref, yc_ref, yd_ref, weights, out_ref, acc_ref, hn_ref, TM_PROJ, final, False)


def _ffn(meta, h, yc, yd, woc, wod, nw, wg, wu, wd, fnw, final):
    rows = h.shape[0]
    c2 = lambda i: (0, 0)
    c3 = lambda i: (0, 0, 0)
    tile = lambda i: (i, 0)
    half = D_MODEL // 2
    meta_specs = [] if final else [
        pl.BlockSpec((BLK, D_MODEL), c2),
        pl.BlockSpec((BLK, half), c2),
        pl.BlockSpec((BLK, half), c2),
    ]
    out_specs = [pl.BlockSpec((TM_PROJ, D_MODEL), tile)]
    out_shape = [jax.ShapeDtypeStruct((rows, D_MODEL), F32)]
    if not final:
        out_specs = [pl.BlockSpec((BLK, D_MODEL), c2)] + out_specs
        out_shape = [jax.ShapeDtypeStruct((BLK, D_MODEL), F32)] + out_shape
    res = pl.pallas_call(
        functools.partial(_ffn_kernel, final=final),
        grid=(rows // TM_PROJ,),
        in_specs=meta_specs + [
            pl.BlockSpec((TM_PROJ, D_MODEL), tile),
            pl.BlockSpec((TM_PROJ, half), tile),
            pl.BlockSpec((TM_PROJ, half), tile),
            pl.BlockSpec((half, D_MODEL), c2, pipeline_mode=pl.Buffered(1)),
            pl.BlockSpec((half, D_MODEL), c2, pipeline_mode=pl.Buffered(1)),
            pl.BlockSpec((1, D_MODEL), c2),
            pl.BlockSpec((N_FF_BLK, D_MODEL, FF_BLK), c3, pipeline_mode=pl.Buffered(1)),
            pl.BlockSpec((N_FF_BLK, D_MODEL, FF_BLK), c3, pipeline_mode=pl.Buffered(1)),
            pl.BlockSpec((N_FF_BLK, FF_BLK, D_MODEL), c3, pipeline_mode=pl.Buffered(1)),
            pl.BlockSpec((1, D_MODEL), c2),
        ],
        out_specs=out_specs,
        out_shape=out_shape,
        scratch_shapes=[pltpu.VMEM((TM_PROJ, D_MODEL), F32), pltpu.VMEM((TM_PROJ, D_MODEL), BF16)],
        compiler_params=pltpu.CompilerParams(
            dimension_semantics=("arbitrary",), vmem_limit_bytes=VMEM_LIMIT),
        name="ffn",
    )(*([] if final else list(meta)), h, yc, yd, woc, wod, nw, wg, wu, wd, fnw)
    return (None, res[0]) if final else (res[0], res[1])


def _split3(x):
    a = x.astype(BF16)
    r = x - a.astype(F32)
    b = r.astype(BF16)
    c = (r - b.astype(F32)).astype(BF16)
    return a, b, c


def _prep_chunk(p_ref, ba_ref, rows, consts, qwin_ref, qkv_ref, is_meta):
    cw_ref, alog_ref, dtb_ref = consts
    qwin_ref[QKV_HALO:QKV_HALO + BLK, :] = p_ref[rows, 0:COL_Z]
    qoff = QKV_HALO - (DN_SHORT_CONV - 1)
    qkv = None
    for j in reversed(range(DN_SHORT_CONV)):
        term = cw_ref[j:j + 1, :] * qwin_ref[qoff + j:qoff + j + BLK, :]
        qkv = term if qkv is None else qkv + term
    qwin_ref[0:QKV_HALO, :] = qwin_ref[BLK:BLK + QKV_HALO, :]
    qkv_ref[...] = qkv * _sigmoid(qkv)

    ba = ba_ref[rows, :]
    beta_all = _sigmoid(ba)
    xg = ba + dtb_ref[...]
    softplus = jnp.maximum(xg, 0.0) + jnp.log(1.0 + jnp.exp(-jnp.abs(xg)))
    g_all = -jnp.exp(alog_ref[...]) * softplus
    if is_meta:
        row = lax.broadcasted_iota(jnp.int32, g_all.shape, 0)
        g_all = jnp.where(row >= FRONT, g_all, 0.0)
    ci = lax.broadcasted_iota(jnp.int32, (BLK, BLK), 0)
    si = lax.broadcasted_iota(jnp.int32, (BLK, BLK), 1)
    same = (ci >= CHUNK) == (si >= CHUNK)
    causal = (si <= ci) & same
    strict = (si < ci) & same
    first = lax.broadcasted_iota(jnp.int32, (1, BLK), 1) < CHUNK
    tri = jnp.where(causal, 1.0, 0.0).astype(BF16)
    g1, g2, g3 = _split3(g_all)
    gcs = _dot(tri, g1) + _dot(tri, g2) + _dot(tri, g3)
    gcs_t = gcs.T

    heads = []
    for h in range(DN_HEADS):
        lo = h * DN_HEAD_DIM
        q = qkv_ref[:, lo:lo + DN_HEAD_DIM]
        k = qkv_ref[:, COL_K + lo:COL_K + lo + DN_HEAD_DIM]
        v = qkv_ref[:, COL_V + lo:COL_V + lo + DN_HEAD_DIM]
        q = q * lax.rsqrt(jnp.sum(q * q, axis=-1, keepdims=True) + 1e-6) * (DN_HEAD_DIM ** -0.5)
        k = k * lax.rsqrt(jnp.sum(k * k, axis=-1, keepdims=True) + 1e-6)
        beta = beta_all[:, h:h + 1]
        col = DN_HEADS + h
        gcol = gcs[:, col:col + 1]
        grow = gcs_t[col:col + 1, :]
        glast = [gcs[CHUNK - 1:CHUNK, col:col + 1], gcs[BLK - 1:BLK, col:col + 1]]
        decay = jnp.exp(jnp.where(causal, gcol - grow, -1e30))
        eg = jnp.exp(gcol)
        kb = k * beta
        k_t = k.T
        kd_t = k_t * jnp.exp(jnp.where(first, glast[0], glast[1]) - grow)
        heads.append(dict(
            head=h, q=q, kb=kb, k_t=k_t.astype(BF16), decay=decay, strict=strict,
            x=jnp.concatenate([v * beta, kb * eg], axis=1),
            qd=q * eg,
            kd_t=[jnp.where(first, kd_t, 0.0), jnp.where(first, 0.0, kd_t)],
            gl=[jnp.broadcast_to(jnp.exp(g), (1, LANES)) for g in glast],
        ))
    return heads


def _solve_chains(chains, pq_ref, bb_ref, oo_ref, gl_ref):
    n = len(chains)
    pk = [_dot(jnp.concatenate([c["kb"], c["q"]], axis=0).astype(BF16), c["k_t"]) for c in chains]
    pw, attn, x = [], [], []
    for c, r in zip(chains, pk):
        pw.append(jnp.where(c["strict"], r[0:BLK] * c["decay"], 0.0))
        attn.append(r[BLK:2 * BLK] * c["decay"])
        x.append(c["x"])
    for s in range(NEUMANN_STAGES):
        last = s == NEUMANN_STAGES - 1
        if last:
            r = [_dot(pw[i].astype(BF16), x[i].astype(BF16)) for i in range(n)]
        else:
            r = [_dot(pw[i].astype(BF16), jnp.concatenate([x[i], pw[i]], axis=1).astype(BF16))
                 for i in range(n)]
        for i in range(n):
            upd = r[i][:, 0:2 * DN_HEAD_DIM]
            x[i] = x[i] - upd if s == 0 else x[i] + upd
            if not last:
                pw[i] = r[i][:, 2 * DN_HEAD_DIM:2 * DN_HEAD_DIM + BLK]
    fin = []
    for i, c in enumerate(chains):
        wu = jnp.concatenate([x[i][:, DN_HEAD_DIM:], x[i][:, :DN_HEAD_DIM]], axis=1).astype(BF16)
        lhs = jnp.concatenate(c["kd_t"] + [attn[i]], axis=0).astype(BF16)
        fin.append(_dot(lhs, wu))
    for i, c in enumerate(chains):
        r = fin[i]
        slot, h = c["slot"], c["head"]
        q_eff = c["qd"] - r[2 * DN_HEAD_DIM:, 0:DN_HEAD_DIM]
        o_loc = r[2 * DN_HEAD_DIM:, DN_HEAD_DIM:]
        for j in range(BLK // CHUNK):
            sub = slot * (BLK // CHUNK) + j
            p_mat = r[j * DN_HEAD_DIM:(j + 1) * DN_HEAD_DIM, 0:DN_HEAD_DIM]
            pq_ref[sub, h] = jnp.concatenate(
                [p_mat, q_eff[j * CHUNK:(j + 1) * CHUNK]], axis=0).astype(BF16)
            bb_ref[sub, h] = r[j * DN_HEAD_DIM:(j + 1) * DN_HEAD_DIM, DN_HEAD_DIM:]
            oo_ref[sub, h] = o_loc[j * CHUNK:(j + 1) * CHUNK]
            gl_ref[sub, h:h + 1, :] = c["gl"][j]


def _state_step(p_ref, y_ref, rows, sub, nrm_ref, pq_ref, bb_ref, oo_ref, gl_ref, s_ref):
    s_old = [s_ref[h] for h in range(DN_HEADS)]
    r = [_dot(pq_ref[sub, h], s_old[h].astype(BF16)) for h in range(DN_HEADS)]
    for h in range(DN_HEADS):
        lo = h * DN_HEAD_DIM
        gl = gl_ref[sub, h:h + 1, :]
        s_ref[h] = s_old[h] * gl + bb_ref[sub, h] - r[h][0:DN_HEAD_DIM]
        o = r[h][DN_HEAD_DIM:] + oo_ref[sub, h]
        z = p_ref[rows, COL_Z + lo:COL_Z + lo + DN_HEAD_DIM]
        o = o * lax.rsqrt(jnp.mean(o * o, axis=-1, keepdims=True) + NORM_EPS) * nrm_ref[...]
        y_ref[rows, lo:lo + DN_HEAD_DIM] = (o * (z * _sigmoid(z))).astype(BF16)


def _mixer_kernel(pm_ref, bam_ref, p_ref, ba_ref, cw_ref, alog_ref, dtb_ref, nrm_ref, ym_ref, y_ref,
                  qwin_ref, qkv_ref, s_ref, pq_ref, bb_ref, oo_ref, gl_ref):
    prep_consts = (cw_ref, alog_ref, dtb_ref)
    solved = (pq_ref, bb_ref, oo_ref, gl_ref)

    @pl.when(pl.program_id(0) == 0)
    def _():
        qwin_ref[...] = jnp.zeros_like(qwin_ref)
        s_ref[...] = jnp.zeros_like(s_ref)
        heads = _prep_chunk(pm_ref, bam_ref, pl.ds(0, BLK), prep_consts, qwin_ref, qkv_ref.at[0], True)
        _solve_chains([dict(c, slot=0) for c in heads], *solved)
        for j in range(BLK // CHUNK):
            _state_step(pm_ref, ym_ref, pl.ds(j * CHUNK, CHUNK), j, nrm_ref, *solved, s_ref)

    def solve_body(g, carry):
        chains = []
        for i in range(GROUP):
            slot = g * GROUP + i
            rows = pl.ds(pl.multiple_of(slot * BLK, BLK), BLK)
            heads = _prep_chunk(p_ref, ba_ref, rows, prep_consts, qwin_ref, qkv_ref.at[i], False)
            chains += [dict(c, slot=slot) for c in heads]
        _solve_chains(chains, *solved)
        return carry

    lax.fori_loop(0, N_SLOT // GROUP, solve_body, 0)

    def state_body(g, carry):
        for i in range(STATE_UNROLL):
            sub = g * STATE_UNROLL + i
            rows = pl.ds(pl.multiple_of(sub * CHUNK, CHUNK), CHUNK)
            _state_step(p_ref, y_ref, rows, sub, nrm_ref, *solved, s_ref)
        return carry

    lax.fori_loop(0, N_SUB // STATE_UNROLL, state_body, 0)


def _mixer(qkvz_m, ba_m, qkvz, ba, cw, alog, dtb, nrm):
    rows = qkvz.shape[0]
    c2 = lambda i: (0, 0)
    tile = lambda i: (i, 0)
    return pl.pallas_call(
        _mixer_kernel,
        grid=(rows // TB_MIX,),
        in_specs=[
            pl.BlockSpec((BLK, D_QKVZ), c2),
            pl.BlockSpec((BLK, LANES), c2),
            pl.BlockSpec((TB_MIX, D_QKVZ), tile),
            pl.BlockSpec((TB_MIX, LANES), tile),
            pl.BlockSpec((2 * DN_SHORT_CONV, 3 * DN_KEY), c2),
            pl.BlockSpec((1, LANES), c2),
            pl.BlockSpec((1, LANES), c2),
            pl.BlockSpec((1, DN_HEAD_DIM), c2),
        ],
        out_specs=[
            pl.BlockSpec((BLK, DN_KEY), c2),
            pl.BlockSpec((TB_MIX, DN_KEY), tile),
        ],
        out_shape=[
            jax.ShapeDtypeStruct((BLK, DN_KEY), BF16),
            jax.ShapeDtypeStruct((rows, DN_KEY), BF16),
        ],
        scratch_shapes=[
            pltpu.VMEM((QKV_HALO + BLK, 3 * DN_KEY), F32),
            pltpu.VMEM((GROUP, BLK, 3 * DN_KEY), F32),
            pltpu.VMEM((DN_HEADS, DN_HEAD_DIM, DN_HEAD_DIM), F32),
            pltpu.VMEM((N_SUB, DN_HEADS, DN_HEAD_DIM + CHUNK, DN_HEAD_DIM), BF16),
            pltpu.VMEM((N_SUB, DN_HEADS, DN_HEAD_DIM, DN_HEAD_DIM), F32),
            pltpu.VMEM((N_SUB, DN_HEADS, CHUNK, DN_HEAD_DIM), F32),
            pltpu.VMEM((N_SUB, SUBLANES, LANES), F32),
        ],
        compiler_params=pltpu.CompilerParams(
            dimension_semantics=("arbitrary",), vmem_limit_bytes=VMEM_LIMIT),
        name="mixer",
    )(qkvz_m, ba_m, qkvz, ba, cw, alog, dtb, nrm)


def kernel(x, meta_tokens, norm_mix_w, w_in, conv_dw_w, conv_dw_b, conv_ln_w, conv_ln_b, dn_conv_w,
           dn_A_log, dn_dt_bias, dn_norm_w, w_out, norm_ffn_w, ffn_w_gu, ffn_w_down, final_norm_w):
    assert x.shape[0] == 1
    h = x[0]
    hm = jnp.concatenate([jnp.zeros((FRONT, D_MODEL), F32), meta_tokens.astype(F32)], axis=0)

    w_in_p = jnp.pad(w_in, ((0, 0), (0, 0), (0, D_IN_PAD - w_in.shape[-1]))).astype(BF16)
    w_out_b = w_out.astype(BF16)
    wg = ffn_w_gu[:, :, :D_FF].reshape(DEPTH, D_MODEL, N_FF_BLK, FF_BLK).transpose(0, 2, 1, 3).astype(BF16)
    wu = ffn_w_gu[:, :, D_FF:].reshape(DEPTH, D_MODEL, N_FF_BLK, FF_BLK).transpose(0, 2, 1, 3).astype(BF16)
    wd = ffn_w_down.reshape(DEPTH, N_FF_BLK, FF_BLK, D_MODEL).astype(BF16)
    dww = jnp.pad(conv_dw_w, ((0, 0), (0, 1), (0, 0)))
    cw = jnp.pad(dn_conv_w, ((0, 0), (0, DN_SHORT_CONV), (0, 0)))
    head_pad = ((0, 0), (DN_HEADS, LANES - 2 * DN_HEADS))
    alog = jnp.pad(dn_A_log, head_pad)
    dtb = jnp.pad(dn_dt_bias, head_pad)
    fnw = final_norm_w.reshape(1, D_MODEL)

    for l in range(DEPTH):
        ycm, qkvzm, bam, yc, qkvz, ba = _inproj(
            hm, h, norm_mix_w[l].reshape(1, D_MODEL), w_in_p[l], dww[l], conv_dw_b[l].reshape(1, -1),
            conv_ln_w[l].reshape(1, -1), conv_ln_b[l].reshape(1, -1))
        ydm, yd = _mixer(qkvzm, bam, qkvz, ba, cw[l], alog[l].reshape(1, -1), dtb[l].reshape(1, -1),
                         dn_norm_w[l].reshape(1, -1))
        hm, h = _ffn((hm, ycm, ydm), h, yc, yd, w_out_b[l, :CONV_CH], w_out_b[l, CONV_CH:],
                     norm_ffn_w[l].reshape(1, D_MODEL), wg[l], wu[l], wd[l], fnw, final=l == DEPTH - 1)
    return h[None]
```
